```python
import math
import jax
import jax.numpy as jnp
from jax import lax
import numpy as np

D_MODEL = 1024
BATCH = 8
SEQ = 2048
DEPTH = 2

HEAD_DIM = 64
ROPE_THETA = 10000.0
RMS_EPS = 1e-6
NEG_INF = -1e30
BLOCK = 128

A_Q_HEADS = 8
A_KV_HEADS = 2
A_GROUP = A_Q_HEADS // A_KV_HEADS
WINDOW = 128
A_Q_WIDTH = A_Q_HEADS * HEAD_DIM
A_KV_WIDTH = A_KV_HEADS * HEAD_DIM

HY_WIDTH = D_MODEL // 2
HY_ORDER = 2
HY_DIRS = 2
HY_SHORT = 3
HY_BANDS = 16
HY_POS_DIM = 1 + 2 * HY_BANDS
HY_FILTER_HIDDEN = 64
HY_DECAY_TARGET = 1e-2
HY_FAST_PCT = 0.3
HY_SLOW_PCT = 1.5

C_HEADS = 4
C_WIDTH = C_HEADS * 2 * HEAD_DIM

N_BRANCH = 3
IN_SIZES = (A_Q_WIDTH, A_KV_WIDTH, A_KV_WIDTH, (HY_ORDER + 1) * HY_WIDTH, C_WIDTH, C_WIDTH, C_WIDTH, N_BRANCH * D_MODEL)
IN_COLS = sum(IN_SIZES)
IN_SPLITS = tuple(sum(IN_SIZES[:i + 1]) for i in range(len(IN_SIZES) - 1))

D_FF = -(-8 * D_MODEL // (3 * 256)) * 256

kernel_name = 'hybrid_gated_window_hyena_diffattn_encoder'


def rms_norm(x, g):
    xf = x.astype(jnp.float32)
    y = xf * lax.rsqrt(jnp.mean(xf * xf, axis=-1, keepdims=True) + RMS_EPS)
    return (y * g.astype(jnp.float32)).astype(x.dtype)


def rope_tables(seq):
    pos = jnp.arange(seq, dtype=jnp.float32)
    inv = ROPE_THETA ** (-jnp.arange(0, HEAD_DIM, 2, dtype=jnp.float32) / HEAD_DIM)
    ang = pos[:, None] * inv[None, :]
    return jnp.cos(ang), jnp.sin(ang)


def apply_rope(x, cos, sin):
    c = cos[None, :, None, :].astype(x.dtype)
    s = sin[None, :, None, :].astype(x.dtype)
    x1, x2 = jnp.split(x, 2, axis=-1)
    return jnp.concatenate([x1 * c - x2 * s, x2 * c + x1 * s], axis=-1)


def window_attention(q, k, v, sink):
    B, S = q.shape[0], q.shape[1]
    nb = S // BLOCK
    span = BLOCK + 2 * WINDOW
    qb = q.reshape(B, nb, BLOCK, A_KV_HEADS, A_GROUP, HEAD_DIM)
    pad = ((0, 0), (WINDOW, WINDOW), (0, 0), (0, 0))
    kp = jnp.pad(k, pad)
    vp = jnp.pad(v, pad)
    idx = jnp.arange(nb)[:, None] * BLOCK + jnp.arange(span)[None, :]
    kb = kp[:, idx]
    vb = vp[:, idx]
    s = jnp.einsum('bnqhgd,bnkhd->bnhgqk', qb, kb).astype(jnp.float32) / math.sqrt(HEAD_DIM)
    q_pos = jnp.arange(nb)[:, None] * BLOCK + jnp.arange(BLOCK)[None, :]
    k_pos = idx - WINDOW
    dist = k_pos[:, None, :] - q_pos[:, :, None]
    valid = (jnp.abs(dist) <= WINDOW) & (k_pos[:, None, :] >= 0) & (k_pos[:, None, :] < S)
    s = jnp.where(valid[None, :, None, None], s, NEG_INF)
    sink_b = jnp.broadcast_to(sink.astype(jnp.float32).reshape(A_KV_HEADS, A_GROUP)[None, None, :, :, None, None], s.shape[:-1] + (1,))
    p = jax.nn.softmax(jnp.concatenate([s, sink_b], axis=-1), axis=-1)[..., :-1]
    o = jnp.einsum('bnhgqk,bnkhd->bnqhgd', p.astype(v.dtype), vb)
    return o.reshape(B, S, A_Q_HEADS * HEAD_DIM)


def diff_attention(q, k, v, lam):
    B, S = q.shape[0], q.shape[1]
    nb = S // BLOCK
    qb = q.reshape(B, nb, BLOCK, C_HEADS, 2, HEAD_DIM).transpose(1, 0, 3, 4, 2, 5)
    kt = k.transpose(0, 2, 3, 1, 4)
    vt = v.transpose(0, 2, 1, 3)

    def one_block(qblk):
        s = jnp.einsum('bhcqd,bhckd->bhcqk', qblk, kt).astype(jnp.float32) / math.sqrt(HEAD_DIM)
        p = jax.nn.softmax(s, axis=-1)
        a = p[:, :, 0] - lam * p[:, :, 1]
        return jnp.einsum('bhqk,bhkd->bhqd', a.astype(v.dtype), vt)

    o = lax.map(one_block, qb)
    return o.transpose(1, 0, 3, 2, 4).reshape(B, S, C_HEADS, 2 * HEAD_DIM)


def short_conv(u, w, b):
    L = u.shape[1]
    up = jnp.pad(u, ((0, 0), (1, 1), (0, 0)))
    return up[:, :L] * w[0] + up[:, 1:L + 1] * w[1] + up[:, 2:] * w[2] + b


def hyena_filters(L, w1, b1, f1, w2, b2, f2, w3):
    f32 = jnp.float32
    pos = jnp.arange(L, dtype=f32)
    t = pos / max(L - 1, 1)
    bands = jnp.arange(1, HY_BANDS + 1, dtype=f32)
    ang = 2.0 * math.pi * pos[:, None] * bands[None, :] / L
    z = jnp.concatenate([t[:, None], jnp.cos(ang), jnp.sin(ang)], axis=-1)
    h = jnp.sin(f1.astype(f32) * (z @ w1.astype(f32) + b1.astype(f32)))
    h = jnp.sin(f2.astype(f32) * (h @ w2.astype(f32) + b2.astype(f32)))
    h = (h @ w3.astype(f32)).reshape(L, HY_ORDER, HY_DIRS, HY_WIDTH)
    min_decay = math.log(HY_DECAY_TARGET) / HY_SLOW_PCT
    max_decay = math.log(HY_DECAY_TARGET) / HY_FAST_PCT
    deltas = jnp.abs(jnp.linspace(min_decay, max_decay, HY_WIDTH, dtype=f32))
    decay = jnp.exp(-t[:, None] * deltas[None, :])
    h = h * decay[:, None, None, :]
    fwd, bwd = h[:, :, 0], h[:, :, 1]
    c = jnp.concatenate([fwd[:1] + bwd[:1], fwd[1:], jnp.zeros_like(fwd[:1]), bwd[1:][::-1]], axis=0)
    return c * lax.rsqrt(jnp.sum(c * c, axis=0, keepdims=True) + RMS_EPS)


def fft_conv(u, cf):
    L = u.shape[1]
    U = jnp.fft.rfft(u.astype(jnp.float32), n=2 * L, axis=1)
    y = jnp.fft.irfft(U * cf[None], n=2 * L, axis=1)[:, :L]
    return y.astype(u.dtype)


def hyena_branch(u, conv_w, conv_b, w1, b1, f1, w2, b2, f2, w3, hy_d):
    L = u.shape[1]
    u = short_conv(u, conv_w, conv_b)
    v, x1, x2 = jnp.split(u, 3, axis=-1)
    cf = jnp.fft.rfft(hyena_filters(L, w1, b1, f1, w2, b2, f2, w3), axis=0)
    z = x1 * (fft_conv(v, cf[:, 0]) + hy_d[0] * v)
    z = x2 * (fft_conv(z, cf[:, 1]) + hy_d[1] * z)
    return z


def setup_inputs(seed: int = 0) -> dict:
    key = jax.random.key(seed)
    ks = jax.random.split(key, 40)

    def nrm(i, shape, scale):
        return jax.random.normal(ks[i], shape, dtype=jnp.float32) * scale

    def gain(i, shape):
        return 1.0 + nrm(i, shape, 0.05)

    L = DEPTH
    return {
        'x': nrm(0, (BATCH, SEQ, D_MODEL), 1.0),
        'norm1_g': gain(1, (L, D_MODEL)),
        'w_in': nrm(2, (L, D_MODEL, IN_COLS), D_MODEL ** -0.5),
        'sink_a': nrm(3, (L, A_Q_HEADS), 0.5),
        'qn_a': gain(4, (L, HEAD_DIM)),
        'kn_a': gain(5, (L, HEAD_DIM)),
        'conv_w': nrm(6, (L, HY_SHORT, (HY_ORDER + 1) * HY_WIDTH), 0.6),
        'conv_b': nrm(7, (L, (HY_ORDER + 1) * HY_WIDTH), 0.01),
        'filt_w1': nrm(8, (L, HY_POS_DIM, HY_FILTER_HIDDEN), HY_POS_DIM ** -0.5),
        'filt_b1': nrm(9, (L, HY_FILTER_HIDDEN), 0.1),
        'filt_f1': gain(10, (L, HY_FILTER_HIDDEN)),
        'filt_w2': nrm(11, (L, HY_FILTER_HIDDEN, HY_FILTER_HIDDEN), HY_FILTER_HIDDEN ** -0.5),
        'filt_b2': nrm(12, (L, HY_FILTER_HIDDEN), 0.1),
        'filt_f2': gain(13, (L, HY_FILTER_HIDDEN)),
        'filt_w3': nrm(14, (L, HY_FILTER_HIDDEN, HY_ORDER * HY_DIRS * HY_WIDTH), HY_FILTER_HIDDEN ** -0.5),
        'hy_d': nrm(15, (L, HY_ORDER, HY_WIDTH), 0.1),
        'qn_c': gain(16, (L, HEAD_DIM)),
        'kn_c': gain(17, (L, HEAD_DIM)),
        'lam_q1': nrm(18, (L, HEAD_DIM), 0.1),
        'lam_k1': nrm(19, (L, HEAD_DIM), 0.1),
        'lam_q2': nrm(20, (L, HEAD_DIM), 0.1),
        'lam_k2': nrm(21, (L, HEAD_DIM), 0.1),
        'subln_c': gain(22, (L, 2 * HEAD_DIM)),
        'w_oa': nrm(23, (L, A_Q_WIDTH, D_MODEL), A_Q_WIDTH ** -0.5),
        'w_ob': nrm(24, (L, HY_WIDTH, D_MODEL), HY_WIDTH ** -0.5),
        'w_oc': nrm(25, (L, C_WIDTH, D_MODEL), C_WIDTH ** -0.5),
        'w_out': nrm(26, (L, D_MODEL, D_MODEL), D_MODEL ** -0.5),
        'norm2_g': gain(27, (L, D_MODEL)),
        'ffn_w1': nrm(28, (L, D_MODEL, D_FF), D_MODEL ** -0.5),
        'ffn_w3': nrm(29, (L, D_MODEL, D_FF), D_MODEL ** -0.5),
        'ffn_w2': nrm(30, (L, D_FF, D_MODEL), D_FF ** -0.5),
    }


def reference(x, norm1_g, w_in, sink_a, qn_a, kn_a, conv_w, conv_b, filt_w1, filt_b1, filt_f1, filt_w2, filt_b2, filt_f2, filt_w3, hy_d, qn_c, kn_c, lam_q1, lam_k1, lam_q2, lam_k2, subln_c, w_oa, w_ob, w_oc, w_out, norm2_g, ffn_w1, ffn_w3, ffn_w2):
    B, S = x.shape[0], x.shape[1]
    cos, sin = rope_tables(S)
    for l in range(DEPTH):
        h = rms_norm(x, norm1_g[l])
        proj = h @ w_in[l]
        qa, ka, va, hy_in, qc, kc, vc, gate_logits = jnp.split(proj, IN_SPLITS, axis=-1)

        qa = apply_rope(rms_norm(qa.reshape(B, S, A_Q_HEADS, HEAD_DIM), qn_a[l]), cos, sin)
        ka = apply_rope(rms_norm(ka.reshape(B, S, A_KV_HEADS, HEAD_DIM), kn_a[l]), cos, sin)
        va = va.reshape(B, S, A_KV_HEADS, HEAD_DIM)
        out_a = window_attention(qa, ka, va, sink_a[l])

        out_b = hyena_branch(hy_in, conv_w[l], conv_b[l], filt_w1[l], filt_b1[l], filt_f1[l], filt_w2[l], filt_b2[l], filt_f2[l], filt_w3[l], hy_d[l])

        lambda_init = 0.8 - 0.6 * math.exp(-0.3 * l)
        lam = (jnp.exp(jnp.sum(lam_q1[l].astype(jnp.float32) * lam_k1[l].astype(jnp.float32)))
               - jnp.exp(jnp.sum(lam_q2[l].astype(jnp.float32) * lam_k2[l].astype(jnp.float32)))
               + lambda_init)
        qc = apply_rope(rms_norm(qc.reshape(B, S, 2 * C_HEADS, HEAD_DIM), qn_c[l]), cos, sin).reshape(B, S, C_HEADS, 2, HEAD_DIM)
        kc = apply_rope(rms_norm(kc.reshape(B, S, 2 * C_HEADS, HEAD_DIM), kn_c[l]), cos, sin).reshape(B, S, C_HEADS, 2, HEAD_DIM)
        vc = vc.reshape(B, S, C_HEADS, 2 * HEAD_DIM)
        out_c = diff_attention(qc, kc, vc, lam)
        out_c = (rms_norm(out_c, subln_c[l]) * (1.0 - lambda_init)).reshape(B, S, C_WIDTH)

        g = jax.nn.sigmoid(gate_logits.astype(jnp.float32)).reshape(B, S, N_BRANCH, D_MODEL).astype(x.dtype)
        merged = (g[:, :, 0] * (out_a @ w_oa[l])
                  + g[:, :, 1] * (out_b @ w_ob[l])
                  + g[:, :, 2] * (out_c @ w_oc[l]))
        x = x + merged @ w_out[l]

        h2 = rms_norm(x, norm2_g[l])
        x = x + (jax.nn.silu(h2 @ ffn_w1[l]) * (h2 @ ffn_w3[l])) @ ffn_w2[l]
    return x
```

```python
import functools
import math

import numpy as np
import jax
import jax.numpy as jnp
from jax import lax
from jax.experimental import pallas as pl
from jax.experimental.pallas import tpu as pltpu

F32 = jnp.float32
BF16 = jnp.bfloat16

HEAD_DIM = 64
HALF = HEAD_DIM // 2
ROPE_THETA = 10000.0
RMS_EPS = 1e-6
NEG_INF = -1e30
WINDOW = 128
A_Q_HEADS = 8
A_KV_HEADS = 2
C_HEADS = 4
HY_ORDER = 2
HY_BANDS = 16
HY_DECAY_TARGET = 1e-2
HY_FAST_PCT = 0.3
HY_SLOW_PCT = 1.5
N_BRANCH = 3

LANES = 128
BF16_SUBLANES = 16
VMEM_LIMIT = 52 * 1024 * 1024


def _cparams(*sem):
    return pltpu.CompilerParams(dimension_semantics=sem, vmem_limit_bytes=VMEM_LIMIT)


def _dot(a, b):
    return jnp.dot(a, b, preferred_element_type=F32)


def _dot_nt(a, b):
    return lax.dot_general(a, b, (((1,), (1,)), ((), ())), preferred_element_type=F32)


_SRC_ORDER = ("qa", "ka", "va", "hy", "qc", "kc", "vc", "gates")
_DST_ORDER = ("gates", "hy", "qc", "kc", "vc", "qa", "ka", "va")
_PAIRED = ("qa", "ka", "qc", "kc")


def _layout(d_model):
    hy_w = (HY_ORDER + 1) * (d_model // 2)
    sizes = dict(qa=A_Q_HEADS * HEAD_DIM, ka=A_KV_HEADS * HEAD_DIM, va=A_KV_HEADS * HEAD_DIM,
                 hy=hy_w, qc=2 * C_HEADS * HEAD_DIM, kc=2 * C_HEADS * HEAD_DIM,
                 vc=2 * C_HEADS * HEAD_DIM, gates=N_BRANCH * d_model)
    dst, off = {}, 0
    for name in _DST_ORDER:
        dst[name] = off
        off += sizes[name]
    return sizes, dst


def _permute_w_in(w, sizes):
    src, off = {}, 0
    for name in _SRC_ORDER:
        src[name] = off
        off += sizes[name]
    parts = []
    for name in _DST_ORDER:
        blk = w[:, src[name]:src[name] + sizes[name]]
        if name in _PAIRED:
            blk = blk.reshape(w.shape[0], -1, 2, 2, HALF).transpose(0, 1, 3, 2, 4)
            blk = blk.reshape(w.shape[0], sizes[name])
        parts.append(blk)
    return jnp.concatenate(parts, axis=1).astype(BF16)


def _pair_gain(g):
    return jnp.concatenate([g[:HALF], g[:HALF], g[HALF:], g[HALF:]]).reshape(1, LANES).astype(F32)


def _rope_tables(seq):
    pos = jnp.arange(seq, dtype=F32)
    inv = ROPE_THETA ** (-jnp.arange(0, HEAD_DIM, 2, dtype=F32) / HEAD_DIM)
    ang = pos[:, None] * inv[None, :]
    c, s = jnp.cos(ang), jnp.sin(ang)
    return jnp.concatenate([c, c, c, c], -1), jnp.concatenate([-s, -s, s, s], -1)


def _head_masks():
    lane = np.arange(LANES)
    in_a = (lane % HEAD_DIM) < HALF
    seg = (in_a[:, None] == in_a[None, :]).astype(np.float32)
    return (jnp.asarray(in_a.astype(np.float32)).reshape(1, LANES),
            jnp.asarray(seg, dtype=BF16))


def _in_proj_kernel(x_ref, g_ref, w_ref, o_ref, xn_ref):
    @pl.when(pl.program_id(1) == 0)
    def _():
        x = x_ref[...]
        inv = lax.rsqrt(jnp.mean(x * x, axis=-1, keepdims=True) + RMS_EPS)
        xn_ref[...] = (x * inv * g_ref[...]).astype(BF16)

    o_ref[...] = _dot(xn_ref[...], w_ref[...]).astype(o_ref.dtype)


def _in_proj(x2d, g, w, tm, tn):
    m, d = x2d.shape
    n = w.shape[1]
    return pl.pallas_call(
        _in_proj_kernel,
        out_shape=jax.ShapeDtypeStruct((m, n), BF16),
        grid=(m // tm, n // tn),
        in_specs=[pl.BlockSpec((tm, d), lambda i, j: (i, 0)),
                  pl.BlockSpec((1, d), lambda i, j: (0, 0)),
                  pl.BlockSpec((d, tn), lambda i, j: (0, j))],
        out_specs=pl.BlockSpec((tm, tn), lambda i, j: (i, j)),
        scratch_shapes=[pltpu.VMEM((tm, d), BF16)],
        compiler_params=_cparams("parallel", "arbitrary"),
        name="in_proj",
    )(x2d, g, w)


def _norm_rope(x, gain, cos, sin, seg, scale):
    ssq = _dot((x * x).astype(BF16), seg)
    y = x * lax.rsqrt(ssq * (1.0 / HEAD_DIM) + RMS_EPS) * gain
    r = y * cos + pltpu.roll(y, LANES // 2, 1) * sin
    return r * scale if scale != 1.0 else r


def _prep_kernel(qa_ref, ka_ref, va_ref, qc_ref, kc_ref, cos_ref, sin_ref, ma_ref, seg_ref,
                 gqa_ref, gka_ref, gqc_ref, gkc_ref,
                 qa_o, ka_o, va_o, qc_o, kc0_o, kc1_o):
    cos, sin, seg = cos_ref[...], sin_ref[...], seg_ref[...]
    m_a = ma_ref[...]
    m_b = 1.0 - m_a
    q_scale = 1.0 / math.sqrt(HEAD_DIM)
    for g in range(qa_ref.shape[1] // LANES):
        sl = slice(g * LANES, (g + 1) * LANES)
        x = qa_ref[:, sl].astype(F32)
        qa_o[:, sl] = _norm_rope(x, gqa_ref[...], cos, sin, seg, q_scale).astype(BF16)
    k = _norm_rope(ka_ref[...].astype(F32), gka_ref[...], cos, sin, seg, 1.0)
    k0a = k * m_a
    k1b = k * m_b
    ka_o[:, 0 * LANES:1 * LANES] = k0a.astype(BF16)
    ka_o[:, 1 * LANES:2 * LANES] = pltpu.roll(k0a, HALF, 1).astype(BF16)
    ka_o[:, 2 * LANES:3 * LANES] = pltpu.roll(k1b, LANES - HALF, 1).astype(BF16)
    ka_o[:, 3 * LANES:4 * LANES] = k1b.astype(BF16)
    lane = lax.broadcasted_iota(jnp.int32, (1, LANES), 1)
    lo = (lane < HEAD_DIM).astype(F32)
    v = va_ref[...].astype(F32)
    v0l = v * lo
    v1r = v * (1.0 - lo)
    va_o[:, 0 * LANES:1 * LANES] = v0l.astype(BF16)
    va_o[:, 1 * LANES:2 * LANES] = pltpu.roll(v0l, HEAD_DIM, 1).astype(BF16)
    va_o[:, 2 * LANES:3 * LANES] = pltpu.roll(v1r, HEAD_DIM, 1).astype(BF16)
    va_o[:, 3 * LANES:4 * LANES] = v1r.astype(BF16)
    for g in range(qc_ref.shape[1] // LANES):
        sl = slice(g * LANES, (g + 1) * LANES)
        q = _norm_rope(qc_ref[:, sl].astype(F32), gqc_ref[...], cos, sin, seg, q_scale)
        qc_o[:, sl] = q.astype(BF16)
        kk = _norm_rope(kc_ref[:, sl].astype(F32), gkc_ref[...], cos, sin, seg, 1.0)
        kc0_o[:, sl] = (kk * m_a).astype(BF16)
        kc1_o[:, sl] = (kk * m_b).astype(BF16)


def _prep(proj, dst, sizes, cos, sin, gqa, gka, gqc, gkc, seq, tm):
    m = proj.shape[0]
    nblk = seq // tm
    mask_a, seg = _head_masks()

    def col(name, width):
        blk = dst[name] // width
        return pl.BlockSpec((tm, width), lambda i: (i, blk))

    tab = pl.BlockSpec((tm, LANES), lambda i: (i % nblk, 0))
    vec = pl.BlockSpec((1, LANES), lambda i: (0, 0))
    wide = sizes["qa"]
    out = lambda w: jax.ShapeDtypeStruct((m, w), BF16)
    ospec = lambda w: pl.BlockSpec((tm, w), lambda i: (i, 0))
    return pl.pallas_call(
        _prep_kernel,
        out_shape=[out(wide), out(4 * LANES), out(4 * LANES), out(sizes["qc"]), out(sizes["kc"]),
                   out(sizes["kc"])],
        grid=(m // tm,),
        in_specs=[col("qa", sizes["qa"]), col("ka", sizes["ka"]), col("va", sizes["va"]),
                  col("qc", sizes["qc"]), col("kc", sizes["kc"]), tab, tab, vec,
                  pl.BlockSpec((LANES, LANES), lambda i: (0, 0)), vec, vec, vec, vec],
        out_specs=[ospec(wide), ospec(4 * LANES), ospec(4 * LANES), ospec(sizes["qc"]),
                   ospec(sizes["kc"]), ospec(sizes["kc"])],
        compiler_params=_cparams("parallel"),
        name="prep_qk",
    )(proj, proj, proj, proj, proj, cos, sin, mask_a, seg, gqa, gka, gqc, gkc)


def _attn_a_kernel(sink_ref, q_ref, k_ref, v_ref, o_ref, *, tq, span, seq):
    i = pl.program_id(1)
    start = jnp.clip(i * tq - WINDOW, 0, seq - span)
    start = pl.multiple_of(start, LANES)
    kwin = k_ref[pl.ds(start, span), :]
    vwin = v_ref[pl.ds(start, span), :]
    qpos = i * tq + lax.broadcasted_iota(jnp.int32, (tq, span), 0)
    kpos = start + lax.broadcasted_iota(jnp.int32, (tq, span), 1)
    valid = jnp.abs(kpos - qpos) <= WINDOW
    group = A_Q_HEADS // A_KV_HEADS
    for g in range(A_Q_HEADS // 2):
        kv = (2 * g) // group
        q = q_ref[:, g * LANES:(g + 1) * LANES]
        acc = None
        for slot in range(2):
            head = 2 * g + slot
            blk = 2 * kv + slot
            s = _dot_nt(q, kwin[:, blk * LANES:(blk + 1) * LANES])
            s = jnp.where(valid, s, NEG_INF)
            sink = sink_ref[head]
            mx = jnp.maximum(jnp.max(s, axis=-1, keepdims=True), sink)
            p = jnp.exp(s - mx)
            den = jnp.sum(p, axis=-1, keepdims=True) + jnp.exp(sink - mx)
            p = (p * (1.0 / den)).astype(BF16)
            part = _dot(p, vwin[:, blk * LANES:(blk + 1) * LANES])
            acc = part if acc is None else acc + part
        o_ref[:, g * LANES:(g + 1) * LANES] = acc.astype(o_ref.dtype)


def _attn_a(sink, qa, ka4, va4, batch, seq, tq):
    span = tq + 2 * WINDOW
    width = qa.shape[-1]
    q3, k3, v3 = (a.reshape(batch, seq, a.shape[-1]) for a in (qa, ka4, va4))
    kern = functools.partial(_attn_a_kernel, tq=tq, span=span, seq=seq)
    out = pl.pallas_call(
        kern,
        out_shape=jax.ShapeDtypeStruct((batch, seq, width), BF16),
        grid=(batch, seq // tq),
        in_specs=[pl.BlockSpec(memory_space=pltpu.SMEM),
                  pl.BlockSpec((None, tq, width), lambda b, i: (b, i, 0)),
                  pl.BlockSpec((None, seq, k3.shape[-1]), lambda b, i: (b, 0, 0)),
                  pl.BlockSpec((None, seq, v3.shape[-1]), lambda b, i: (b, 0, 0))],
        out_specs=pl.BlockSpec((None, tq, width), lambda b, i: (b, i, 0)),
        compiler_params=_cparams("parallel", "arbitrary"),
        name="attn_window",
    )(sink, q3, k3, v3)
    return out.reshape(batch * seq, width)


def _attn_c_kernel(lam_ref, q_ref, k0_ref, k1_ref, v_ref, g_ref, o_ref, *, lambda_init):
    lv = lam_ref[...]
    lam = (jnp.exp(jnp.sum(lv[0:1] * lv[1:2], axis=-1, keepdims=True))
           - jnp.exp(jnp.sum(lv[2:3] * lv[3:4], axis=-1, keepdims=True)) + lambda_init)
    q = q_ref[...]

    def probs(k_ref_):
        s = _dot_nt(q, k_ref_[...])
        p = jnp.exp(s - jnp.max(s, axis=-1, keepdims=True))
        return p, jnp.sum(p, axis=-1, keepdims=True)

    p0, l0 = probs(k0_ref)
    p1, l1 = probs(k1_ref)
    a = p0 * (1.0 / l0) - p1 * (lam / l1)
    o = _dot(a.astype(BF16), v_ref[...])
    inv = lax.rsqrt(jnp.mean(o * o, axis=-1, keepdims=True) + RMS_EPS)
    o_ref[...] = (o * inv * g_ref[...] * (1.0 - lambda_init)).astype(o_ref.dtype)


def _attn_c(lam_vecs, qc, kc0, kc1, proj, vc_off, subln, batch, seq, tq, lambda_init):
    width = qc.shape[-1]
    hw = 2 * HEAD_DIM
    q3, k03, k13 = (a.reshape(batch, seq, width) for a in (qc, kc0, kc1))
    p3 = proj.reshape(batch, seq, proj.shape[-1])
    vblk = vc_off // hw
    kern = functools.partial(_attn_c_kernel, lambda_init=lambda_init)
    kspec = pl.BlockSpec((None, seq, hw), lambda b, h, i: (b, 0, h))
    out = pl.pallas_call(
        kern,
        out_shape=jax.ShapeDtypeStruct((batch, seq, width), BF16),
        grid=(batch, width // hw, seq // tq),
        in_specs=[pl.BlockSpec(lam_vecs.shape, lambda b, h, i: (0, 0)),
                  pl.BlockSpec((None, tq, hw), lambda b, h, i: (b, i, h)),
                  kspec, kspec,
                  pl.BlockSpec((None, seq, hw), lambda b, h, i: (b, 0, vblk + h)),
                  pl.BlockSpec((1, hw), lambda b, h, i: (0, 0))],
        out_specs=pl.BlockSpec((None, tq, hw), lambda b, h, i: (b, i, h)),
        compiler_params=_cparams("parallel", "parallel", "arbitrary"),
        name="attn_diff",
    )(lam_vecs, q3, k03, k13, p3, subln)
    return out.reshape(batch * seq, width)


def _short_conv_kernel(u_ref, w_ref, b_ref, o_ref):
    u = u_ref[...].astype(F32)
    n = u.shape[0]
    row = lax.broadcasted_iota(jnp.int32, u.shape, 0)
    prev = jnp.where(row == 0, 0.0, pltpu.roll(u, 1, 0))
    nxt = jnp.where(row == n - 1, 0.0, pltpu.roll(u, n - 1, 0))
    w = w_ref[...]
    o_ref[...] = (prev * w[0:1] + u * w[1:2] + nxt * w[2:3] + b_ref[...]).astype(o_ref.dtype)


def _short_conv(proj, hy_off, conv_w, conv_b, batch, seq, tc):
    p3 = proj.reshape(batch, seq, proj.shape[-1])
    width = conv_w.shape[-1]
    blk0 = hy_off // tc
    return pl.pallas_call(
        _short_conv_kernel,
        out_shape=jax.ShapeDtypeStruct((batch, seq, width), BF16),
        grid=(batch, width // tc),
        in_specs=[pl.BlockSpec((None, seq, tc), lambda b, c: (b, 0, blk0 + c)),
                  pl.BlockSpec((conv_w.shape[0], tc), lambda b, c: (0, c)),
                  pl.BlockSpec((1, tc), lambda b, c: (0, c))],
        out_specs=pl.BlockSpec((None, seq, tc), lambda b, c: (b, 0, c)),
        compiler_params=_cparams("parallel", "parallel"),
        name="hy_short_conv",
    )(p3, conv_w, conv_b.reshape(1, width))


def _filter_kernel(z_ref, w1_ref, b1_ref, f1_ref, w2_ref, b2_ref, f2_ref, w3f_ref, w3b_ref,
                   dl_ref, fs_ref, fd_ref, *, seq):
    h = jnp.sin(f1_ref[...] * (_dot(z_ref[...], w1_ref[...]) + b1_ref[...]))
    h = jnp.sin(f2_ref[...] * (_dot(h, w2_ref[...]) + b2_ref[...]))
    row = lax.broadcasted_iota(jnp.int32, (seq, 1), 0)
    t = row.astype(F32) / max(seq - 1, 1)
    decay = jnp.exp(-t * dl_ref[...])
    fwd = _dot(h, w3f_ref[...]) * decay
    bwd = _dot(h, w3b_ref[...]) * decay
    first = row == 0
    c0 = jnp.where(first, fwd + bwd, fwd)
    cb = jnp.where(first, 0.0, bwd)
    nrm = lax.rsqrt(jnp.sum(c0 * c0 + cb * cb, axis=0, keepdims=True) + RMS_EPS)
    fs_ref[...] = ((c0 + cb) * nrm).astype(fs_ref.dtype)
    fd_ref[...] = ((c0 - cb) * nrm).astype(fd_ref.dtype)


def _filters(z, w1, b1, f1, w2, b2, f2, w3, deltas, seq, width):
    hid = w2.shape[0]
    zp = z.shape[1]
    full = lambda shape: pl.BlockSpec(shape, lambda o: (0,) * len(shape))
    kern = functools.partial(_filter_kernel, seq=seq)
    out = jax.ShapeDtypeStruct((seq, HY_ORDER * width), BF16)
    return pl.pallas_call(
        kern,
        out_shape=[out, out],
        grid=(HY_ORDER,),
        in_specs=[full((seq, zp)), full((zp, hid)), full((1, hid)), full((1, hid)),
                  full((hid, hid)), full((1, hid)), full((1, hid)),
                  pl.BlockSpec((hid, width), lambda o: (0, 2 * o)),
                  pl.BlockSpec((hid, width), lambda o: (0, 2 * o + 1)),
                  full((1, width))],
        out_specs=[pl.BlockSpec((seq, width), lambda o: (0, o))] * 2,
        compiler_params=_cparams("parallel"),
        name="hy_filters",
    )(z, w1, b1, f1, w2, b2, f2, w3, w3, deltas)


def _spectrum_kernel(fre_ref, fim_ref, s_ref, d_ref, cre_ref, cim_ref):
    cre_ref[...] = _dot(fre_ref[...], s_ref[...])
    cim_ref[...] = _dot(fim_ref[...], d_ref[...])


def _spectrum(fre, fim, fs, fd, tk, tc):
    kp, seq = fre.shape
    n = fs.shape[1]
    out = jax.ShapeDtypeStruct((kp, n), F32)
    return pl.pallas_call(
        _spectrum_kernel,
        out_shape=[out, out],
        grid=(kp // tk, n // tc),
        in_specs=[pl.BlockSpec((tk, seq), lambda k, c: (k, 0)),
                  pl.BlockSpec((tk, seq), lambda k, c: (k, 0)),
                  pl.BlockSpec((seq, tc), lambda k, c: (0, c)),
                  pl.BlockSpec((seq, tc), lambda k, c: (0, c))],
        out_specs=[pl.BlockSpec((tk, tc), lambda k, c: (k, c))] * 2,
        compiler_params=_cparams("parallel", "parallel"),
        name="hy_filter_spectrum",
    )(fre, fim, fs, fd)


def _dft_fwd_kernel(fre_ref, fim_ref, u_ref, cre_ref, cim_ref, yre_ref, yim_ref):
    u = u_ref[...]
    ure = _dot(fre_ref[...], u)
    uim = _dot(fim_ref[...], u)
    cre, cim = cre_ref[...], cim_ref[...]
    yre_ref[...] = (ure * cre - uim * cim).astype(yre_ref.dtype)
    yim_ref[...] = (ure * cim + uim * cre).astype(yim_ref.dtype)


def _dft_fwd(fre, fim, u3, ublk0, cre, cim, cblk0, width, tk, tc):
    kp, seq = fre.shape
    batch = u3.shape[0]
    out = jax.ShapeDtypeStruct((batch, kp, width), BF16)
    return pl.pallas_call(
        _dft_fwd_kernel,
        out_shape=[out, out],
        grid=(kp // tk, batch, width // tc),
        in_specs=[pl.BlockSpec((tk, seq), lambda k, b, c: (k, 0)),
                  pl.BlockSpec((tk, seq), lambda k, b, c: (k, 0)),
                  pl.BlockSpec((None, seq, tc), lambda k, b, c: (b, 0, ublk0 + c)),
                  pl.BlockSpec((tk, tc), lambda k, b, c: (k, cblk0 + c)),
                  pl.BlockSpec((tk, tc), lambda k, b, c: (k, cblk0 + c))],
        out_specs=[pl.BlockSpec((None, tk, tc), lambda k, b, c: (b, k, c))] * 2,
        compiler_params=_cparams("parallel", "parallel", "parallel"),
        name="hy_dft_fwd",
    )(fre, fim, u3, cre, cim)


def _dft_inv_kernel(gre_ref, gim_ref, yre_ref, yim_ref, u_ref, x_ref, d_ref, o_ref):
    y = _dot(gre_ref[...], yre_ref[...]) + _dot(gim_ref[...], yim_ref[...])
    u = u_ref[...].astype(F32)
    o_ref[...] = (x_ref[...].astype(F32) * (y + d_ref[...] * u)).astype(o_ref.dtype)


def _dft_inv(gre, gim, yre, yim, u3, ublk0, x3, xblk0, d, width, ts, tc):
    seq, kp = gre.shape
    batch = yre.shape[0]
    return pl.pallas_call(
        _dft_inv_kernel,
        out_shape=jax.ShapeDtypeStruct((batch, seq, width), BF16),
        grid=(seq // ts, batch, width // tc),
        in_specs=[pl.BlockSpec((ts, kp), lambda t, b, c: (t, 0)),
                  pl.BlockSpec((ts, kp), lambda t, b, c: (t, 0)),
                  pl.BlockSpec((None, kp, tc), lambda t, b, c: (b, 0, c)),
                  pl.BlockSpec((None, kp, tc), lambda t, b, c: (b, 0, c)),
                  pl.BlockSpec((None, ts, tc), lambda t, b, c: (b, t, ublk0 + c)),
                  pl.BlockSpec((None, ts, tc), lambda t, b, c: (b, t, xblk0 + c)),
                  pl.BlockSpec((1, tc), lambda t, b, c: (0, c))],
        out_specs=pl.BlockSpec((None, ts, tc), lambda t, b, c: (b, t, c)),
        compiler_params=_cparams("parallel", "parallel", "parallel"),
        name="hy_dft_inv",
    )(gre, gim, yre, yim, u3, x3, d)


def _dft_tables(seq, kp):
    n = 2 * seq
    k = jnp.arange(kp, dtype=jnp.int32)[:, None]
    s = jnp.arange(seq, dtype=jnp.int32)[None, :]
    ang = ((k * s) % n).astype(F32) * (2.0 * math.pi / n)
    live = k <= seq
    fre = jnp.where(live, jnp.cos(ang), 0.0)
    fim = jnp.where(live, -jnp.sin(ang), 0.0)
    wk = jnp.where((k == 0) | (k == seq), 1.0, 2.0) / n
    return (fre.astype(BF16), fim.astype(BF16),
            (fre * wk).T.astype(BF16), (fim * wk).T.astype(BF16))


def _filter_features(seq, pad):
    pos = jnp.arange(seq, dtype=F32)
    t = pos / max(seq - 1, 1)
    bands = jnp.arange(1, HY_BANDS + 1, dtype=F32)
    ang = 2.0 * math.pi * pos[:, None] * bands[None, :] / seq
    z = jnp.concatenate([t[:, None], jnp.cos(ang), jnp.sin(ang)], axis=-1)
    return jnp.pad(z, ((0, 0), (0, pad - z.shape[1])))


def _merge_kernel(x_ref, a_ref, b_ref, c_ref, ga_ref, gb_ref, gc_ref,
                  wa_ref, wb_ref, wc_ref, wo_ref, o_ref):
    def branch(g_ref, y_ref, w_ref):
        return jax.nn.sigmoid(g_ref[...].astype(F32)) * _dot(y_ref[...], w_ref[...])

    merged = (branch(ga_ref, a_ref, wa_ref) + branch(gb_ref, b_ref, wb_ref)
              + branch(gc_ref, c_ref, wc_ref))
    o_ref[...] = x_ref[...] + _dot(merged.astype(BF16), wo_ref[...])


def _merge(x2d, out_a, out_b, out_c, proj, gates_off, w_oa, w_ob, w_oc, w_out, tm):
    m, d = x2d.shape
    gblk = gates_off // d
    row = lambda w: pl.BlockSpec((tm, w), lambda i: (i, 0))
    gate = lambda r: pl.BlockSpec((tm, d), lambda i: (i, gblk + r))
    full = lambda a: pl.BlockSpec(a.shape, lambda i: (0, 0))
    return pl.pallas_call(
        _merge_kernel,
        out_shape=jax.ShapeDtypeStruct((m, d), F32),
        grid=(m // tm,),
        in_specs=[row(d), row(out_a.shape[1]), row(out_b.shape[1]), row(out_c.shape[1]),
                  gate(0), gate(1), gate(2), full(w_oa), full(w_ob), full(w_oc), full(w_out)],
        out_specs=row(d),
        compiler_params=_cparams("parallel"),
        name="merge_out_proj",
    )(x2d, out_a, out_b, out_c, proj, proj, proj, w_oa, w_ob, w_oc, w_out)


def _ffn_kernel(x_ref, g_ref, w1_ref, w3_ref, w2_ref, o_ref, xn_ref):
    f = pl.program_id(1)

    @pl.when(f == 0)
    def _():
        x = x_ref[...]
        inv = lax.rsqrt(jnp.mean(x * x, axis=-1, keepdims=True) + RMS_EPS)
        xn_ref[...] = (x * inv * g_ref[...]).astype(BF16)
        o_ref[...] = x

    xn = xn_ref[...]
    h = jax.nn.silu(_dot(xn, w1_ref[...])) * _dot(xn, w3_ref[...])
    o_ref[...] += _dot(h.astype(BF16), w2_ref[...])


def _ffn(x2d, g, w1, w3, w2, tm, tf):
    m, d = x2d.shape
    dff = w1.shape[1]
    return pl.pallas_call(
        _ffn_kernel,
        out_shape=jax.ShapeDtypeStruct((m, d), F32),
        grid=(m // tm, dff // tf),
        in_specs=[pl.BlockSpec((tm, d), lambda i, f: (i, 0)),
                  pl.BlockSpec((1, d), lambda i, f: (0, 0)),
                  pl.BlockSpec((d, tf), lambda i, f: (0, f)),
                  pl.BlockSpec((d, tf), lambda i, f: (0, f)),
                  pl.BlockSpec((tf, d), lambda i, f: (f, 0))],
        out_specs=pl.BlockSpec((tm, d), lambda i, f: (i, 0)),
        scratch_shapes=[pltpu.VMEM((tm, d), BF16)],
        compiler_params=_cparams("parallel", "arbitrary"),
        name="ffn_swiglu",
    )(x2d, g, w1, w3, w2)


def _tiles(batch, seq, d_model, in_cols, d_ff):
    kp = -(-(seq + 1) // LANES) * LANES
    return dict(
        proj_tm=min(1024, seq), proj_tn=in_cols // 9 if in_cols % (9 * LANES) == 0 else in_cols,
        prep_tm=min(512, seq),
        attn_a_tq=128,
        attn_c_tq=min(256, seq),
        hy_tc=256, kp=kp, hy_tk=kp // 2, hy_ts=seq // 2,
        merge_tm=min(512, seq),
        ffn_tm=min(1024, seq), ffn_tf=d_ff // 2,
    )


def kernel(x, norm1_g, w_in, sink_a, qn_a, kn_a, conv_w, conv_b, filt_w1, filt_b1, filt_f1,
           filt_w2, filt_b2, filt_f2, filt_w3, hy_d, qn_c, kn_c, lam_q1, lam_k1, lam_q2, lam_k2,
           subln_c, w_oa, w_ob, w_oc, w_out, norm2_g, ffn_w1, ffn_w3, ffn_w2):
    batch, seq, d_model = x.shape
    depth = w_in.shape[0]
    hy_width = d_model // 2
    sizes, dst = _layout(d_model)
    t = _tiles(batch, seq, d_model, w_in.shape[2], ffn_w1.shape[2])

    cos, sin = _rope_tables(seq)
    fre, fim, gre, gim = _dft_tables(seq, t["kp"])
    zpad = LANES
    z = _filter_features(seq, zpad)
    min_decay = math.log(HY_DECAY_TARGET) / HY_SLOW_PCT
    max_decay = math.log(HY_DECAY_TARGET) / HY_FAST_PCT
    deltas = jnp.abs(jnp.linspace(min_decay, max_decay, hy_width, dtype=F32)).reshape(1, hy_width)

    x2d = x.reshape(batch * seq, d_model)
    for l in range(depth):
        lambda_init = 0.8 - 0.6 * math.exp(-0.3 * l)
        w_in_l = _permute_w_in(w_in[l], sizes)
        proj = _in_proj(x2d, norm1_g[l].reshape(1, d_model), w_in_l, t["proj_tm"], t["proj_tn"])

        qa, ka4, va4, qc, kc0, kc1 = _prep(
            proj, dst, sizes, cos, sin, _pair_gain(qn_a[l]), _pair_gain(kn_a[l]),
            _pair_gain(qn_c[l]), _pair_gain(kn_c[l]), seq, t["prep_tm"])

        out_a = _attn_a(sink_a[l].astype(F32), qa, ka4, va4, batch, seq, t["attn_a_tq"])

        lam_vecs = jnp.stack([lam_q1[l], lam_k1[l], lam_q2[l], lam_k2[l]]).astype(F32)
        out_c = _attn_c(lam_vecs, qc, kc0, kc1, proj, dst["vc"],
                        subln_c[l].reshape(1, 2 * HEAD_DIM).astype(F32), batch, seq,
                        t["attn_c_tq"], lambda_init)

        tc = t["hy_tc"]
        u3 = _short_conv(proj, dst["hy"], conv_w[l], conv_b[l], batch, seq, tc)
        w1p = jnp.pad(filt_w1[l], ((0, zpad - filt_w1.shape[1]), (0, 0)))
        row = lambda v: v.reshape(1, -1).astype(F32)
        fs, fd = _filters(z, w1p, row(filt_b1[l]), row(filt_f1[l]), filt_w2[l], row(filt_b2[l]),
                          row(filt_f2[l]), filt_w3[l], deltas, seq, hy_width)
        cre, cim = _spectrum(fre, fim, fs, fd, t["hy_tk"], tc)
        nblk = hy_width // tc
        yre, yim = _dft_fwd(fre, fim, u3, 0, cre, cim, 0, hy_width, t["hy_tk"], tc)
        z1 = _dft_inv(gre, gim, yre, yim, u3, 0, u3, nblk, row(hy_d[l, 0]), hy_width,
                      t["hy_ts"], tc)
        yre, yim = _dft_fwd(fre, fim, z1, 0, cre, cim, nblk, hy_width, t["hy_tk"], tc)
        out_b = _dft_inv(gre, gim, yre, yim, z1, 0, u3, 2 * nblk, row(hy_d[l, 1]), hy_width,
                         t["hy_ts"], tc)
        out_b = out_b.reshape(batch * seq, hy_width)

        x2d = _merge(x2d, out_a, out_b, out_c, proj, dst["gates"], w_oa[l].astype(BF16),
                     w_ob[l].astype(BF16), w_oc[l].astype(BF16), w_out[l].astype(BF16),
                     t["merge_tm"])
        x2d = _ffn(x2d, norm2_g[l].reshape(1, d_model), ffn_w1[l].astype(BF16),
                   ffn_w3[l].astype(BF16), ffn_w2[l].astype(BF16), t["ffn_tm"], t["ffn_tf"])
    return x2d.reshape(batch, seq, d_model)
```
